```python
import math
import jax, jax.numpy as jnp
from jax import lax
import numpy as np

D_MODEL = 2048
BATCH = 16
SEQ = 2048
DEPTH = 2

N_MIXERS = 2
N_MAMBA = (DEPTH + 1) // 2
N_RET = DEPTH // 2
CHUNK = 128

SSM_EXPAND = 2
SSM_D_INNER = SSM_EXPAND * D_MODEL
SSM_HEADDIM = 64
SSM_HEADS = SSM_D_INNER // SSM_HEADDIM
SSM_STATE = 128
SSM_GROUPS = 8
SSM_HPG = SSM_HEADS // SSM_GROUPS
SSM_CONV = 4
SSM_CONV_DIM = SSM_D_INNER + 2 * SSM_GROUPS * SSM_STATE
SSM_IN_DIM = SSM_D_INNER + SSM_CONV_DIM + SSM_HEADS

RET_HEADS = D_MODEL // 256
RET_DK = 256
RET_DV = 512
RET_QK_DIM = RET_HEADS * RET_DK
RET_V_DIM = RET_HEADS * RET_DV
RET_IN_DIM = 2 * RET_QK_DIM + 2 * RET_V_DIM
ROPE_BASE = 10000.0

PEER_HEADS = 8
PEER_NKEYS = 128
PEER_N_EXPERTS = PEER_NKEYS * PEER_NKEYS
PEER_TOPK = 16
PEER_DQ = 256
PEER_DHALF = PEER_DQ // 2
PEER_BLOCK = 128

DN_ALPHA = (2 * DEPTH) ** 0.25
DN_BETA = (8 * DEPTH) ** -0.25
LN_EPS = 1e-5

kernel_name = "hybrid_ssd_retention_peer_deepnorm"


def layer_norm(x, w, b):
    xf = x.astype(jnp.float32)
    mu = jnp.mean(xf, axis=-1, keepdims=True)
    var = jnp.mean(jnp.square(xf - mu), axis=-1, keepdims=True)
    y = (xf - mu) * lax.rsqrt(var + LN_EPS)
    return (y * w + b).astype(x.dtype)


def to_chunks(a):
    b, s = a.shape[:2]
    return a.reshape(b, s // CHUNK, CHUNK, *a.shape[2:]).swapaxes(0, 1)


def from_chunks(a):
    nc, b, l = a.shape[:3]
    return a.swapaxes(0, 1).reshape(b, nc * l, *a.shape[3:])


def causal_dwconv(x, w, b):
    y = lax.conv_general_dilated(
        x, w[:, None, :], window_strides=(1,), padding=[(SSM_CONV - 1, 0)],
        dimension_numbers=("NWC", "WIO", "NWC"), feature_group_count=x.shape[-1])
    return y + b


def ssd_chunked(x_dt, dA, Bm, Cm):
    bsz = x_dt.shape[0]
    mask = jnp.tril(jnp.ones((CHUNK, CHUNK), bool))

    def step(state, inp):
        xc, ac, bc, cc = inp
        cum = jnp.cumsum(ac, axis=1)
        seg = cum[:, :, None] - cum[:, None, :]
        decay = jnp.exp(jnp.where(mask[None, :, :, None, None], seg, -jnp.inf))
        cb = jnp.einsum("blgn,bsgn->blsg", cc, bc)
        y_diag = jnp.einsum("blsg,blsgr,bsgrp->blgrp", cb, decay, xc)
        y_off = jnp.einsum("blgn,bgrpn,blgr->blgrp", cc, state, jnp.exp(cum))
        last = cum[:, -1]
        w_in = jnp.exp(last[:, None] - cum)
        new_state = (state * jnp.exp(last)[..., None, None]
                     + jnp.einsum("bsgn,bsgr,bsgrp->bgrpn", bc, w_in, xc))
        return new_state, y_diag + y_off

    init = jnp.zeros((bsz, SSM_GROUPS, SSM_HPG, SSM_HEADDIM, SSM_STATE), jnp.float32)
    _, y = lax.scan(step, init, (to_chunks(x_dt), to_chunks(dA), to_chunks(Bm), to_chunks(Cm)))
    return from_chunks(y)


def mamba2_mixer(x, in_proj, conv_w, conv_b, dt_bias, A_log, D_skip, norm_w, out_proj):
    bsz, s, _ = x.shape
    zxbcdt = x @ in_proj
    z, xbc, dt = jnp.split(zxbcdt, [SSM_D_INNER, SSM_D_INNER + SSM_CONV_DIM], axis=-1)
    xbc = jax.nn.silu(causal_dwconv(xbc, conv_w, conv_b))
    xs, Bm, Cm = jnp.split(xbc, [SSM_D_INNER, SSM_D_INNER + SSM_GROUPS * SSM_STATE], axis=-1)
    xs = xs.reshape(bsz, s, SSM_GROUPS, SSM_HPG, SSM_HEADDIM).astype(jnp.float32)
    Bm = Bm.reshape(bsz, s, SSM_GROUPS, SSM_STATE).astype(jnp.float32)
    Cm = Cm.reshape(bsz, s, SSM_GROUPS, SSM_STATE).astype(jnp.float32)
    dt = jax.nn.softplus(dt.astype(jnp.float32) + dt_bias.astype(jnp.float32))
    dt = dt.reshape(bsz, s, SSM_GROUPS, SSM_HPG)
    A = -jnp.exp(A_log.astype(jnp.float32)).reshape(SSM_GROUPS, SSM_HPG)
    y = ssd_chunked(xs * dt[..., None], dt * A, Bm, Cm)
    y = y + D_skip.astype(jnp.float32).reshape(SSM_GROUPS, SSM_HPG, 1) * xs
    y = y.reshape(bsz, s, SSM_D_INNER) * jax.nn.silu(z.astype(jnp.float32))
    yg = y.reshape(bsz, s, SSM_GROUPS, SSM_D_INNER // SSM_GROUPS)
    yg = yg * lax.rsqrt(jnp.mean(jnp.square(yg), axis=-1, keepdims=True) + LN_EPS)
    y = yg.reshape(bsz, s, SSM_D_INNER) * norm_w
    return y.astype(x.dtype) @ out_proj


def rotary(t, cos, sin):
    half = t.shape[-1] // 2
    t1, t2 = t[..., :half], t[..., half:]
    return jnp.concatenate([t1 * cos - t2 * sin, t2 * cos + t1 * sin], axis=-1)


def retention_mixer(x, positions, in_proj, gn_w, gn_b, out_proj):
    bsz, s, _ = x.shape
    q, k, v, g = jnp.split(x @ in_proj, [RET_QK_DIM, 2 * RET_QK_DIM, 2 * RET_QK_DIM + RET_V_DIM], axis=-1)
    inv_freq = ROPE_BASE ** (-jnp.arange(0, RET_DK, 2, dtype=jnp.float32) / RET_DK)
    ang = positions.astype(jnp.float32)[..., None] * inv_freq
    cos, sin = jnp.cos(ang)[:, :, None], jnp.sin(ang)[:, :, None]
    q = rotary(q.reshape(bsz, s, RET_HEADS, RET_DK).astype(jnp.float32), cos, sin) * (RET_DK ** -0.5)
    k = rotary(k.reshape(bsz, s, RET_HEADS, RET_DK).astype(jnp.float32), cos, sin)
    v = v.reshape(bsz, s, RET_HEADS, RET_DV).astype(jnp.float32)

    log_gamma = jnp.log1p(-jnp.exp2(-5.0 - jnp.arange(RET_HEADS, dtype=jnp.float32)))
    idx = jnp.arange(CHUNK, dtype=jnp.float32)
    mask = jnp.tril(jnp.ones((CHUNK, CHUNK), bool))
    rel = idx[:, None] - idx[None, :]
    dmat = jnp.exp(jnp.where(mask[:, :, None], rel[:, :, None] * log_gamma, -jnp.inf))
    q_decay = jnp.exp((idx[:, None] + 1.0) * log_gamma)
    k_decay = jnp.exp((CHUNK - 1.0 - idx)[:, None] * log_gamma)
    chunk_decay = jnp.exp(CHUNK * log_gamma)

    def step(R, inp):
        qc, kc, vc = inp
        scores = jnp.einsum("blhd,bshd->blsh", qc, kc) * dmat
        intra = jnp.einsum("blsh,bshv->blhv", scores, vc)
        inter = jnp.einsum("blhd,bhdv->blhv", qc, R) * q_decay[:, :, None]
        R_new = R * chunk_decay[:, None, None] + jnp.einsum("bshd,sh,bshv->bhdv", kc, k_decay, vc)
        return R_new, intra + inter

    R0 = jnp.zeros((bsz, RET_HEADS, RET_DK, RET_DV), jnp.float32)
    _, y = lax.scan(step, R0, (to_chunks(q), to_chunks(k), to_chunks(v)))
    y = from_chunks(y)
    mu = jnp.mean(y, axis=-1, keepdims=True)
    var = jnp.mean(jnp.square(y - mu), axis=-1, keepdims=True)
    y = ((y - mu) * lax.rsqrt(var + LN_EPS)).reshape(bsz, s, RET_V_DIM) * gn_w + gn_b
    y = jax.nn.silu(g.astype(jnp.float32)) * y
    return y.astype(x.dtype) @ out_proj


def peer_ffn(x, w_q, sub_keys, u, v):
    bsz, s, d = x.shape
    xt = x.reshape(-1, PEER_BLOCK, d)

    def block(xb):
        q = (xb @ w_q).reshape(PEER_BLOCK, PEER_HEADS, 2, PEER_DHALF)
        sc = jnp.einsum("thcd,hckd->thck", q, sub_keys)
        s_top, i_top = lax.top_k(sc, PEER_TOPK)
        cand = (s_top[:, :, 0, :, None] + s_top[:, :, 1, None, :]).reshape(PEER_BLOCK, PEER_HEADS, -1)
        cand_idx = (i_top[:, :, 0, :, None] * PEER_NKEYS + i_top[:, :, 1, None, :]).reshape(PEER_BLOCK, PEER_HEADS, -1)
        best, pos = lax.top_k(cand, PEER_TOPK)
        eidx = jnp.take_along_axis(cand_idx, pos, axis=-1)
        gate = jax.nn.softmax(best.astype(jnp.float32), axis=-1)
        ue = jnp.take(u, eidx, axis=0)
        ve = jnp.take(v, eidx, axis=0)
        act = jax.nn.gelu(jnp.einsum("td,thkd->thk", xb, ue).astype(jnp.float32), approximate=False)
        return jnp.einsum("thk,thkd->td", (gate * act).astype(x.dtype), ve)

    return lax.map(block, xt).reshape(bsz, s, d)


def _normal(k, shape, scale):
    return jax.random.normal(k, shape, jnp.float32) * scale


def setup_inputs(seed: int = 0) -> dict:
    key = jax.random.key(seed)
    ks = jax.random.split(key, 24)
    x = _normal(ks[0], (BATCH, SEQ, D_MODEL), 1.0)
    positions = (jax.random.randint(ks[1], (BATCH, 1), 0, 1024, dtype=jnp.int32)
                 + jnp.arange(SEQ, dtype=jnp.int32)[None, :])
    ssm_in_proj = _normal(ks[2], (N_MAMBA, D_MODEL, SSM_IN_DIM), D_MODEL ** -0.5)
    ssm_conv_w = _normal(ks[3], (N_MAMBA, SSM_CONV, SSM_CONV_DIM), SSM_CONV ** -0.5)
    ssm_conv_b = _normal(ks[4], (N_MAMBA, SSM_CONV_DIM), 0.01)
    dt0 = jnp.exp(jax.random.uniform(ks[5], (N_MAMBA, SSM_HEADS), jnp.float32)
                  * (math.log(0.1) - math.log(1e-3)) + math.log(1e-3))
    ssm_dt_bias = dt0 + jnp.log(-jnp.expm1(-dt0))
    ssm_A_log = jnp.log(jax.random.uniform(ks[6], (N_MAMBA, SSM_HEADS), jnp.float32, 1.0, 16.0))
    ssm_D = 1.0 + _normal(ks[7], (N_MAMBA, SSM_HEADS), 0.01)
    ssm_norm_w = 1.0 + _normal(ks[8], (N_MAMBA, SSM_D_INNER), 0.01)
    ssm_out_proj = _normal(ks[9], (N_MAMBA, SSM_D_INNER, D_MODEL), DN_BETA * SSM_D_INNER ** -0.5)
    ret_in_proj = _normal(ks[10], (N_RET, D_MODEL, RET_IN_DIM), D_MODEL ** -0.5)
    ret_gn_w = 1.0 + _normal(ks[11], (N_RET, RET_V_DIM), 0.01)
    ret_gn_b = _normal(ks[12], (N_RET, RET_V_DIM), 0.01)
    ret_out_proj = _normal(ks[13], (N_RET, RET_V_DIM, D_MODEL), DN_BETA * RET_V_DIM ** -0.5)
    mix_ln_w = 1.0 + _normal(ks[14], (DEPTH, D_MODEL), 0.01)
    mix_ln_b = _normal(ks[15], (DEPTH, D_MODEL), 0.01)
    peer_w_q = _normal(ks[16], (DEPTH, D_MODEL, PEER_HEADS * PEER_DQ), D_MODEL ** -0.5)
    peer_sub_keys = _normal(ks[17], (DEPTH, PEER_HEADS, 2, PEER_NKEYS, PEER_DHALF), PEER_DHALF ** -0.5)
    peer_u = _normal(ks[18], (DEPTH, PEER_N_EXPERTS, D_MODEL), D_MODEL ** -0.5)
    peer_v = _normal(ks[19], (DEPTH, PEER_N_EXPERTS, D_MODEL), DN_BETA * PEER_HEADS ** -0.5)
    ffn_ln_w = 1.0 + _normal(ks[20], (DEPTH, D_MODEL), 0.01)
    ffn_ln_b = _normal(ks[21], (DEPTH, D_MODEL), 0.01)
    return {"x": x, "positions": positions,
            "ssm_in_proj": ssm_in_proj, "ssm_conv_w": ssm_conv_w, "ssm_conv_b": ssm_conv_b,
            "ssm_dt_bias": ssm_dt_bias, "ssm_A_log": ssm_A_log, "ssm_D": ssm_D,
            "ssm_norm_w": ssm_norm_w, "ssm_out_proj": ssm_out_proj,
            "ret_in_proj": ret_in_proj, "ret_gn_w": ret_gn_w, "ret_gn_b": ret_gn_b,
            "ret_out_proj": ret_out_proj,
            "mix_ln_w": mix_ln_w, "mix_ln_b": mix_ln_b,
            "peer_w_q": peer_w_q, "peer_sub_keys": peer_sub_keys, "peer_u": peer_u, "peer_v": peer_v,
            "ffn_ln_w": ffn_ln_w, "ffn_ln_b": ffn_ln_b}


def reference(x, positions, ssm_in_proj, ssm_conv_w, ssm_conv_b, ssm_dt_bias, ssm_A_log, ssm_D,
              ssm_norm_w, ssm_out_proj, ret_in_proj, ret_gn_w, ret_gn_b, ret_out_proj,
              mix_ln_w, mix_ln_b, peer_w_q, peer_sub_keys, peer_u, peer_v, ffn_ln_w, ffn_ln_b):
    for i in range(DEPTH):
        j = i // N_MIXERS
        if i % N_MIXERS == 0:
            h = mamba2_mixer(x, ssm_in_proj[j], ssm_conv_w[j], ssm_conv_b[j], ssm_dt_bias[j],
                             ssm_A_log[j], ssm_D[j], ssm_norm_w[j], ssm_out_proj[j])
        else:
            h = retention_mixer(x, positions, ret_in_proj[j], ret_gn_w[j], ret_gn_b[j], ret_out_proj[j])
        x = layer_norm(DN_ALPHA * x + h, mix_ln_w[i], mix_ln_b[i])
        f = peer_ffn(x, peer_w_q[i], peer_sub_keys[i], peer_u[i], peer_v[i])
        x = layer_norm(DN_ALPHA * x + f, ffn_ln_w[i], ffn_ln_b[i])
    return x
```

```python
import functools
import math

import jax
import jax.numpy as jnp
from jax import lax
from jax.experimental import pallas as pl
from jax.experimental.pallas import tpu as pltpu

F32 = jnp.float32
BF16 = jnp.bfloat16

D_MODEL = 2048
DEPTH = 2
CHUNK = 128
SUBLANES = 8

SSM_D_INNER = 4096
SSM_HEADDIM = 64
SSM_HEADS = 64
SSM_STATE = 128
SSM_GROUPS = 8
SSM_HPG = 8
SSM_CONV = 4
SSM_BC = SSM_GROUPS * SSM_STATE
SSM_CONV_DIM = SSM_D_INNER + 2 * SSM_BC
SSM_GROUP_W = SSM_D_INNER // SSM_GROUPS

RET_HEADS = 8
RET_DK = 256
RET_DV = 512
RET_QK_DIM = RET_HEADS * RET_DK
RET_V_DIM = RET_HEADS * RET_DV
ROPE_BASE = 10000.0

PEER_HEADS = 8
PEER_NKEYS = 128
PEER_N_EXPERTS = PEER_NKEYS * PEER_NKEYS
PEER_TOPK = 16
PEER_DHALF = 128

DN_ALPHA = (2 * DEPTH) ** 0.25
LN_EPS = 1e-5
NEG_INF = float("-inf")

VMEM_LIMIT_BYTES = 56 * 1024 * 1024


def _params(*sem):
    return pltpu.CompilerParams(dimension_semantics=sem, vmem_limit_bytes=VMEM_LIMIT_BYTES)


def _sigmoid(x):
    return 1.0 / (1.0 + jnp.exp(-x))


def _to_feature_major_kernel(x_ref, o_ref):
    o_ref[...] = x_ref[0].T


def to_feature_major(x, ts=512, td=512):
    b, s, d = x.shape
    ts = min(ts, s)
    nsb = s // ts
    return pl.pallas_call(
        _to_feature_major_kernel,
        grid=(b, nsb, d // td),
        in_specs=[pl.BlockSpec((1, ts, td), lambda bi, si, di: (bi, si, di))],
        out_specs=pl.BlockSpec((td, ts), lambda bi, si, di: (di, bi * nsb + si)),
        out_shape=jax.ShapeDtypeStruct((d, b * s), x.dtype),
        compiler_params=_params("parallel", "parallel", "parallel"),
        name="to_feature_major",
    )(x)


def _proj_kernel(w_ref, a_ref, o_ref):
    a = a_ref[...].astype(BF16)
    o_ref[...] = jnp.dot(w_ref[...], a, preferred_element_type=F32).astype(o_ref.dtype)


def proj(w_t, a_t, out_dtype, tn=1024, tt=512, name="proj"):
    n, k = w_t.shape
    t = a_t.shape[1]
    tn = min(tn, n)
    tt = min(tt, t)
    return pl.pallas_call(
        _proj_kernel,
        grid=(t // tt, n // tn),
        in_specs=[pl.BlockSpec((tn, k), lambda j, i: (i, 0)),
                  pl.BlockSpec((k, tt), lambda j, i: (0, j))],
        out_specs=pl.BlockSpec((tn, tt), lambda j, i: (i, j)),
        out_shape=jax.ShapeDtypeStruct((n, t), out_dtype),
        compiler_params=_params("parallel", "arbitrary"),
        name=name,
    )(w_t, a_t)


def _add_ln_kernel(x_ref, h_ref, w_ref, b_ref, o_ref, obf_ref):
    y = DN_ALPHA * x_ref[...] + h_ref[...]
    mu = jnp.mean(y, axis=0, keepdims=True)
    d = y - mu
    var = jnp.mean(d * d, axis=0, keepdims=True)
    out = d * lax.rsqrt(var + LN_EPS) * w_ref[...] + b_ref[...]
    o_ref[...] = out
    obf_ref[...] = out.astype(BF16)


def add_ln(x_t, h_t, w, b, tt=256):
    d, t = x_t.shape
    tt = min(tt, t)
    blk = pl.BlockSpec((d, tt), lambda j: (0, j))
    col = pl.BlockSpec((d, 1), lambda j: (0, 0))
    return pl.pallas_call(
        _add_ln_kernel,
        grid=(t // tt,),
        in_specs=[blk, blk, col, col],
        out_specs=[blk, blk],
        out_shape=[jax.ShapeDtypeStruct((d, t), F32), jax.ShapeDtypeStruct((d, t), BF16)],
        compiler_params=_params("parallel"),
        name="add_ln",
    )(x_t, h_t, w.reshape(d, 1), b.reshape(d, 1))


def _add_ln_out_kernel(x_ref, h_ref, w_ref, b_ref, o_ref):
    y = DN_ALPHA * x_ref[...] + h_ref[...]
    mu = jnp.mean(y, axis=0, keepdims=True)
    d = y - mu
    var = jnp.mean(d * d, axis=0, keepdims=True)
    out = d * lax.rsqrt(var + LN_EPS) * w_ref[...] + b_ref[...]
    o_ref[0] = out.T


def add_ln_out(x_t, h_t, w, b, batch, seq, tt=256):
    d, t = x_t.shape
    tt = min(tt, seq)
    nsb = seq // tt
    blk = pl.BlockSpec((d, tt), lambda j: (0, j))
    col = pl.BlockSpec((d, 1), lambda j: (0, 0))
    return pl.pallas_call(
        _add_ln_out_kernel,
        grid=(t // tt,),
        in_specs=[blk, blk, col, col],
        out_specs=pl.BlockSpec((1, tt, d), lambda j: (j // nsb, j % nsb, 0)),
        out_shape=jax.ShapeDtypeStruct((batch, seq, d), F32),
        compiler_params=_params("parallel"),
        name="add_ln_out",
    )(x_t, h_t, w.reshape(d, 1), b.reshape(d, 1))


CONV_ROWS = 256


def _ssd_kernel(dskip_ref, z_ref, xbc_ref, dt_ref, cw_ref, cb_ref, dtb_ref, alog_ref, nw_ref,
                o_ref,
                state_ref, prev_ref, act_ref, cwb_ref, cumT_ref, dtT_ref, ecumT_ref, wdtT_ref,
                elast_ref):
    first = (pl.program_id(0) == 0) & (pl.program_id(1) == 0)

    @pl.when(first)
    def _():
        for k in range(SSM_CONV):
            cwb_ref[k] = jnp.broadcast_to(cw_ref[:, k:k + 1], (SSM_CONV_DIM, CHUNK))
        cwb_ref[SSM_CONV] = jnp.broadcast_to(cb_ref[...], (SSM_CONV_DIM, CHUNK))

    @pl.when(pl.program_id(1) == 0)
    def _():
        state_ref[...] = jnp.zeros_like(state_ref)
        prev_ref[...] = jnp.zeros_like(prev_ref)

    def conv_body(i, carry):
        rows = pl.ds(pl.multiple_of(i * CONV_ROWS, CONV_ROWS), CONV_ROWS)
        cur = xbc_ref[rows, :].astype(F32)
        prev = prev_ref[rows, :]
        lane = lax.broadcasted_iota(jnp.int32, cur.shape, 1)
        acc = cwb_ref[SSM_CONV, rows, :] + cwb_ref[SSM_CONV - 1, rows, :] * cur
        for j in range(1, SSM_CONV):
            shifted = pltpu.roll(jnp.where(lane >= CHUNK - j, prev, cur), j, axis=1)
            acc = acc + cwb_ref[SSM_CONV - 1 - j, rows, :] * shifted
        prev_ref[rows, :] = cur
        act_ref[rows, :] = acc * _sigmoid(acc)
        return carry

    lax.fori_loop(0, SSM_CONV_DIM // CONV_ROWS, conv_body, 0)

    x = dt_ref[...] + dtb_ref[...]
    dtT = jnp.maximum(x, 0.0) + jnp.log1p(jnp.exp(-jnp.abs(x)))
    a_neg = -jnp.exp(alog_ref[...])
    dAT = dtT * a_neg
    row = lax.broadcasted_iota(jnp.int32, (CHUNK, CHUNK), 0)
    col = lax.broadcasted_iota(jnp.int32, (CHUNK, CHUNK), 1)
    upper = (row <= col).astype(F32)
    cumT = jnp.dot(dAT, upper, preferred_element_type=F32, precision=lax.Precision.HIGHEST)
    cum = cumT.T
    lastb = jnp.broadcast_to(cumT[:, CHUNK - 1:CHUNK], (CHUNK, CHUNK))
    cumT_ref[...] = cumT
    dtT_ref[...] = dtT
    ecumT_ref[...] = jnp.exp(cumT)
    wdtT_ref[...] = jnp.exp(lastb - cumT) * dtT
    elast_ref[...] = jnp.exp(lastb)
    causal = row <= col

    def group_body(g, carry):
        b_rows = pl.ds(pl.multiple_of(SSM_D_INNER + g * SSM_STATE, SSM_STATE), SSM_STATE)
        c_rows = pl.ds(pl.multiple_of(SSM_D_INNER + SSM_BC + g * SSM_STATE, SSM_STATE), SSM_STATE)
        b_nat = act_ref[b_rows, :].T.astype(BF16)
        c_t = act_ref[c_rows, :].astype(BF16)
        cb_t = jnp.dot(b_nat, c_t, preferred_element_type=F32)
        st = state_ref[g]
        y_off = jnp.dot(st.astype(BF16), c_t, preferred_element_type=F32)
        cum_g = pltpu.roll(cum, (CHUNK - g * SSM_HPG) % CHUNK, axis=1)
        ys, xws, scales = [], [], []
        for j in range(SSM_HPG):
            r = g * SSM_HPG + j
            r1 = pl.ds(r, 1)
            xrows = pl.ds(pl.multiple_of(r * SSM_HEADDIM, SSM_HEADDIM), SSM_HEADDIM)
            seg = cumT_ref[r1, :] - cum_g[:, j:j + 1]
            dec = jnp.exp(jnp.where(causal, seg, NEG_INF))
            m_t = (cb_t * dec).astype(BF16)
            xr = act_ref[xrows, :]
            y_diag = jnp.dot((xr * dtT_ref[r1, :]).astype(BF16), m_t, preferred_element_type=F32)
            y = y_diag + y_off[j * SSM_HEADDIM:(j + 1) * SSM_HEADDIM, :] * ecumT_ref[r1, :]
            y = y + dskip_ref[r] * xr
            zr = z_ref[xrows, :].astype(F32)
            ys.append(y * (zr * _sigmoid(zr)))
            xws.append((xr * wdtT_ref[r1, :]).astype(BF16))
            scales.append(jnp.broadcast_to(elast_ref[r1, :], (SSM_HEADDIM, CHUNK)))
        yg = jnp.concatenate(ys, axis=0)
        ms = jnp.mean(yg * yg, axis=0, keepdims=True)
        g_rows = pl.ds(pl.multiple_of(g * SSM_GROUP_W, SSM_GROUP_W), SSM_GROUP_W)
        o_ref[g_rows, :] = (yg * lax.rsqrt(ms + LN_EPS) * nw_ref[g_rows, :]).astype(o_ref.dtype)
        upd = jnp.dot(jnp.concatenate(xws, axis=0), b_nat, preferred_element_type=F32)
        state_ref[g] = st * jnp.concatenate(scales, axis=0) + upd
        return carry

    lax.fori_loop(0, SSM_GROUPS, group_body, 0)


def ssd_mixer(z_t, xbc_t, dt_t, conv_w, conv_b, dt_bias, a_log, d_skip, norm_w, batch, seq):
    t = z_t.shape[1]
    nch = seq // CHUNK
    pad = CHUNK - SSM_HEADS
    dtb = jnp.broadcast_to(jnp.pad(dt_bias, (0, pad))[:, None], (CHUNK, CHUNK))
    alog = jnp.broadcast_to(jnp.pad(a_log, (0, pad))[:, None], (CHUNK, CHUNK))
    tok = lambda rows: pl.BlockSpec((rows, CHUNK), lambda b, c: (0, b * nch + c))
    full = lambda shape: pl.BlockSpec(shape, lambda b, c: (0,) * len(shape))
    return pl.pallas_call(
        _ssd_kernel,
        grid=(batch, nch),
        in_specs=[pl.BlockSpec(memory_space=pltpu.SMEM),
                  tok(SSM_D_INNER), tok(SSM_CONV_DIM), tok(CHUNK),
                  full((SSM_CONV_DIM, SSM_CONV)), full((SSM_CONV_DIM, 1)),
                  full((CHUNK, CHUNK)), full((CHUNK, CHUNK)), full((SSM_D_INNER, 1))],
        out_specs=tok(SSM_D_INNER),
        out_shape=jax.ShapeDtypeStruct((SSM_D_INNER, t), BF16),
        scratch_shapes=[
            pltpu.VMEM((SSM_GROUPS, SSM_GROUP_W, SSM_STATE), F32),
            pltpu.VMEM((SSM_CONV_DIM, CHUNK), F32),
            pltpu.VMEM((SSM_CONV_DIM, CHUNK), F32),
            pltpu.VMEM((SSM_CONV + 1, SSM_CONV_DIM, CHUNK), F32),
            pltpu.VMEM((CHUNK, CHUNK), F32),
            pltpu.VMEM((CHUNK, CHUNK), F32),
            pltpu.VMEM((CHUNK, CHUNK), F32),
            pltpu.VMEM((CHUNK, CHUNK), F32),
            pltpu.VMEM((CHUNK, CHUNK), F32),
        ],
        compiler_params=_params("arbitrary", "arbitrary"),
        name="ssd_mixer",
    )(d_skip, z_t, xbc_t, dt_t, conv_w.T, conv_b.reshape(-1, 1), dtb, alog, norm_w.reshape(-1, 1))


def _retention_kernel(cdec_ref, q_ref, k_ref, v_ref, g_ref, pos_ref, invf_ref, dmat_ref, qdec_ref,
                      kdec_ref, gw_ref, gb_ref, o_ref, r_ref):
    @pl.when(pl.program_id(1) == 0)
    def _():
        r_ref[...] = jnp.zeros_like(r_ref)

    half = RET_DK // 2
    ang = invf_ref[...] * pos_ref[...]
    cos = jnp.cos(ang)
    sin = jnp.sin(ang)

    def head_body(h, carry):
        qk_rows = pl.multiple_of(h * RET_DK, RET_DK)
        q1 = q_ref[pl.ds(qk_rows, half), :].astype(F32)
        q2 = q_ref[pl.ds(qk_rows + half, half), :].astype(F32)
        k1 = k_ref[pl.ds(qk_rows, half), :].astype(F32)
        k2 = k_ref[pl.ds(qk_rows + half, half), :].astype(F32)
        scale = RET_DK ** -0.5
        qr1 = ((q1 * cos - q2 * sin) * scale).astype(BF16)
        qr2 = ((q2 * cos + q1 * sin) * scale).astype(BF16)
        kr1 = k1 * cos - k2 * sin
        kr2 = k2 * cos + k1 * sin
        kn1 = kr1.T.astype(BF16)
        kn2 = kr2.T.astype(BF16)
        scores_t = (jnp.dot(kn1, qr1, preferred_element_type=F32)
                    + jnp.dot(kn2, qr2, preferred_element_type=F32))
        p_t = (scores_t * dmat_ref[h]).astype(BF16)
        v_rows = pl.ds(pl.multiple_of(h * RET_DV, RET_DV), RET_DV)
        v_t = v_ref[v_rows, :].astype(F32)
        intra = jnp.dot(v_t.astype(BF16), p_t, preferred_element_type=F32)
        h1 = pl.ds(h, 1)
        r1 = r_ref[h, 0]
        r2 = r_ref[h, 1]
        inter = (jnp.dot(r1.astype(BF16), qr1, preferred_element_type=F32)
                 + jnp.dot(r2.astype(BF16), qr2, preferred_element_type=F32)) * qdec_ref[h1, :]
        vk = (v_t * kdec_ref[h1, :]).astype(BF16)
        cd = cdec_ref[h]
        r_ref[h, 0] = r1 * cd + jnp.dot(vk, kn1, preferred_element_type=F32)
        r_ref[h, 1] = r2 * cd + jnp.dot(vk, kn2, preferred_element_type=F32)
        y = intra + inter
        mu = jnp.mean(y, axis=0, keepdims=True)
        d = y - mu
        var = jnp.mean(d * d, axis=0, keepdims=True)
        yn = d * lax.rsqrt(var + LN_EPS) * gw_ref[v_rows, :] + gb_ref[v_rows, :]
        gt = g_ref[v_rows, :].astype(F32)
        o_ref[v_rows, :] = (gt * _sigmoid(gt) * yn).astype(o_ref.dtype)
        return carry

    lax.fori_loop(0, RET_HEADS, head_body, 0)


def retention_mixer(qkvg_t, pos_f, gn_w, gn_b, batch, seq):
    t = qkvg_t.shape[1]
    nch = seq // CHUNK
    half = RET_DK // 2
    inv_freq = ROPE_BASE ** (-jnp.arange(0, RET_DK, 2, dtype=F32) / RET_DK)
    invf = jnp.broadcast_to(inv_freq[:, None], (half, CHUNK))
    log_gamma = jnp.log1p(-jnp.exp2(-5.0 - jnp.arange(RET_HEADS, dtype=F32)))
    idx = jnp.arange(CHUNK, dtype=F32)
    rel = idx[None, :] - idx[:, None]
    dmat_t = jnp.exp(jnp.where(rel[None] >= 0, rel[None] * log_gamma[:, None, None], NEG_INF))
    qdec = jnp.exp((idx[None, :] + 1.0) * log_gamma[:, None])
    kdec = jnp.exp((CHUNK - 1.0 - idx)[None, :] * log_gamma[:, None])
    cdec = jnp.exp(CHUNK * log_gamma)
    tok = lambda rows, rb: pl.BlockSpec((rows, CHUNK), lambda b, c: (rb, b * nch + c))
    full = lambda shape: pl.BlockSpec(shape, lambda b, c: (0,) * len(shape))
    return pl.pallas_call(
        _retention_kernel,
        grid=(batch, nch),
        in_specs=[pl.BlockSpec(memory_space=pltpu.SMEM),
                  tok(RET_QK_DIM, 0), tok(RET_QK_DIM, 1), tok(RET_V_DIM, 1), tok(RET_V_DIM, 2),
                  pl.BlockSpec((1, CHUNK), lambda b, c: (0, b * nch + c)),
                  full((half, CHUNK)), full((RET_HEADS, CHUNK, CHUNK)),
                  full((RET_HEADS, CHUNK)), full((RET_HEADS, CHUNK)),
                  full((RET_V_DIM, 1)), full((RET_V_DIM, 1))],
        out_specs=pl.BlockSpec((RET_V_DIM, CHUNK), lambda b, c: (0, b * nch + c)),
        out_shape=jax.ShapeDtypeStruct((RET_V_DIM, t), BF16),
        scratch_shapes=[pltpu.VMEM((RET_HEADS, 2, RET_DV, half), F32)],
        compiler_params=_params("arbitrary", "arbitrary"),
        name="retention_mixer",
    )(cdec, qkvg_t, qkvg_t, qkvg_t, qkvg_t, pos_f, invf, dmat_t, qdec, kdec,
      gn_w.reshape(-1, 1), gn_b.reshape(-1, 1))


def _knock_out_max(vals):
    m = jnp.max(vals, axis=0, keepdims=True)
    return m, jnp.where(vals == m, NEG_INF, vals)


def _peer_route_kernel(q_ref, keys_ref, s0_ref, s1_ref, a_ref, b_ref, tau_ref):
    def head_body(h, carry):
        tops = []
        scores = []
        for c in range(2):
            rows = pl.ds(pl.multiple_of((2 * h + c) * PEER_DHALF, PEER_DHALF), PEER_DHALF)
            sc = jnp.dot(keys_ref[2 * h + c], q_ref[rows, :].astype(BF16),
                         preferred_element_type=F32)
            scores.append(sc)
            work = sc
            top = []
            for _ in range(PEER_TOPK):
                m, work = _knock_out_max(work)
                top.append(m)
            tops.append(top)
        top1 = jnp.concatenate(tops[1], axis=0)
        cand = jnp.concatenate([tops[0][k] + top1 for k in range(PEER_TOPK)], axis=0)
        best, cand = _knock_out_max(cand)
        zsum = jnp.ones_like(best)
        m = best
        for _ in range(PEER_TOPK - 1):
            m, cand = _knock_out_max(cand)
            zsum = zsum + jnp.exp(m - best)
        s0_ref[h] = scores[0]
        s1_ref[h] = scores[1]
        a_ref[h] = jnp.exp(scores[0] - tops[0][0])
        b_ref[h] = jnp.exp(scores[1] - tops[1][0]) / zsum
        tau_ref[pl.ds(h, 1), :] = m
        return carry

    lax.fori_loop(0, PEER_HEADS, head_body, 0)


def peer_route(q_t, keys_bf, tt=256):
    t = q_t.shape[1]
    tt = min(tt, t)
    big = pl.BlockSpec((PEER_HEADS, PEER_NKEYS, tt), lambda j: (0, 0, j))
    big_shape = jax.ShapeDtypeStruct((PEER_HEADS, PEER_NKEYS, t), F32)
    return pl.pallas_call(
        _peer_route_kernel,
        grid=(t // tt,),
        in_specs=[pl.BlockSpec((q_t.shape[0], tt), lambda j: (0, j)),
                  pl.BlockSpec(keys_bf.shape, lambda j: (0, 0, 0))],
        out_specs=[big, big, big, big, pl.BlockSpec((PEER_HEADS, tt), lambda j: (0, j))],
        out_shape=[big_shape, big_shape, big_shape, big_shape,
                   jax.ShapeDtypeStruct((PEER_HEADS, t), F32)],
        compiler_params=_params("parallel"),
        name="peer_route",
    )(q_t, keys_bf)


def _gelu_exact(x):
    return 0.5 * x * (1.0 + lax.erf(x * (2.0 ** -0.5)))


def _peer_dense_kernel(x_ref, u_ref, vt_ref, s0_ref, s1_ref, a_ref, b_ref, tau_ref, o_ref,
                       h_ref, act_ref, *, te, tt):
    e = pl.program_id(1)
    h_ref[...] = jnp.dot(u_ref[...], x_ref[...], preferred_element_type=F32)
    i1_rows = pl.ds(pl.multiple_of(e * SUBLANES, SUBLANES), SUBLANES)

    def lane_tile_body(lt, carry):
        lanes = pl.ds(pl.multiple_of(lt * CHUNK, CHUNK), CHUNK)
        s0_rows = [s0_ref[hd, i1_rows, lanes] for hd in range(PEER_HEADS)]
        a_rows = [a_ref[hd, i1_rows, lanes] for hd in range(PEER_HEADS)]
        taus = tau_ref[:, lanes]
        for ii in range(te // PEER_NKEYS):
            rows = slice(ii * PEER_NKEYS, (ii + 1) * PEER_NKEYS)
            w = jnp.zeros((PEER_NKEYS, CHUNK), F32)
            for hd in range(PEER_HEADS):
                picked = (s1_ref[hd, :, lanes] + s0_rows[hd][ii:ii + 1, :]) >= taus[hd:hd + 1, :]
                w = w + jnp.where(picked, b_ref[hd, :, lanes], 0.0) * a_rows[hd][ii:ii + 1, :]
            act_ref[rows, lanes] = (_gelu_exact(h_ref[rows, lanes]) * w).astype(BF16)
        return carry

    lax.fori_loop(0, tt // CHUNK, lane_tile_body, 0)
    contrib = jnp.dot(vt_ref[...], act_ref[...], preferred_element_type=F32)

    @pl.when(e == 0)
    def _():
        o_ref[...] = contrib

    @pl.when(e != 0)
    def _():
        o_ref[...] += contrib


def peer_dense(x_bf, u_bf, vt_bf, s0, s1, a, b, tau, tt=512):
    d, t = x_bf.shape
    n_e = u_bf.shape[0]
    te = SUBLANES * PEER_NKEYS
    tt = min(tt, t)
    once = pl.Buffered(1)
    big = pl.BlockSpec((PEER_HEADS, PEER_NKEYS, tt), lambda j, e: (0, 0, j), pipeline_mode=once)
    return pl.pallas_call(
        functools.partial(_peer_dense_kernel, te=te, tt=tt),
        grid=(t // tt, n_e // te),
        in_specs=[pl.BlockSpec((d, tt), lambda j, e: (0, j), pipeline_mode=once),
                  pl.BlockSpec((te, d), lambda j, e: (e, 0)),
                  pl.BlockSpec((d, te), lambda j, e: (0, e)),
                  big, big, big, big,
                  pl.BlockSpec((PEER_HEADS, tt), lambda j, e: (0, j), pipeline_mode=once)],
        out_specs=pl.BlockSpec((d, tt), lambda j, e: (0, j)),
        out_shape=jax.ShapeDtypeStruct((d, t), F32),
        scratch_shapes=[pltpu.VMEM((te, tt), F32), pltpu.VMEM((te, tt), BF16)],
        compiler_params=_params("parallel", "arbitrary"),
        name="peer_dense",
    )(x_bf, u_bf, vt_bf, s0, s1, a, b, tau)


def peer_ffn(x_f32, x_bf, w_q, sub_keys, u, v):
    q_t = proj(w_q.T.astype(BF16), x_bf, F32, name="peer_query")
    keys_bf = sub_keys.reshape(2 * PEER_HEADS, PEER_NKEYS, PEER_DHALF).astype(BF16)
    s0, s1, a, b, tau = peer_route(q_t, keys_bf)
    return peer_dense(x_bf, u.astype(BF16), v.T.astype(BF16), s0, s1, a, b, tau)


def kernel(x, positions, ssm_in_proj, ssm_conv_w, ssm_conv_b, ssm_dt_bias, ssm_A_log, ssm_D, ssm_norm_w, ssm_out_proj, ret_in_proj, ret_gn_w, ret_gn_b, ret_out_proj, mix_ln_w, mix_ln_b, peer_w_q, peer_sub_keys, peer_u, peer_v, ffn_ln_w, ffn_ln_b):
    batch, seq, _ = x.shape
    x_t = to_feature_major(x)
    x_bf = x_t
    pos_f = positions.astype(F32).reshape(1, batch * seq)
    out = None
    for i in range(DEPTH):
        j = i // 2
        if i % 2 == 0:
            w_in = ssm_in_proj[j].T.astype(BF16)
            z_t = proj(w_in[:SSM_D_INNER], x_bf, BF16, name="ssm_in_z")
            xbc_t = proj(w_in[SSM_D_INNER:SSM_D_INNER + SSM_CONV_DIM], x_bf, BF16, name="ssm_in_xbc")
            w_dt = jnp.pad(w_in[SSM_D_INNER + SSM_CONV_DIM:], ((0, CHUNK - SSM_HEADS), (0, 0)))
            dt_t = proj(w_dt, x_bf, F32, name="ssm_in_dt")
            y_t = ssd_mixer(z_t, xbc_t, dt_t, ssm_conv_w[j], ssm_conv_b[j], ssm_dt_bias[j],
                            ssm_A_log[j], ssm_D[j], ssm_norm_w[j], batch, seq)
            h_t = proj(ssm_out_proj[j].T.astype(BF16), y_t, F32, name="ssm_out")
        else:
            qkvg_t = proj(ret_in_proj[j].T.astype(BF16), x_bf, BF16, name="ret_in")
            y_t = retention_mixer(qkvg_t, pos_f, ret_gn_w[j], ret_gn_b[j], batch, seq)
            h_t = proj(ret_out_proj[j].T.astype(BF16), y_t, F32, name="ret_out")
        x_t, x_bf = add_ln(x_t, h_t, mix_ln_w[i], mix_ln_b[i])
        f_t = peer_ffn(x_t, x_bf, peer_w_q[i], peer_sub_keys[i], peer_u[i], peer_v[i])
        if i == DEPTH - 1:
            out = add_ln_out(x_t, f_t, ffn_ln_w[i], ffn_ln_b[i], batch, seq)
        else:
            x_t, x_bf = add_ln(x_t, f_t, ffn_ln_w[i], ffn_ln_b[i])
    return out
```
